```python
import jax, jax.numpy as jnp
from jax import lax
import numpy as np

D_MODEL = 2048
BATCH = 8
SEQ = 2048
DEPTH = 1

CHUNK = 64
N_META = 16
D_MIX = D_MODEL
D_CONV = D_MIX // 2
CONV_GROUPS = 8
CONV_K = 3
D_ATTN = D_MIX - D_CONV
N_ATTN_HEADS = 8
ATTN_HEAD_DIM = D_ATTN // N_ATTN_HEADS
HEAD_DIM_OUT = D_CONV // CONV_GROUPS
Q_BLOCK = 128
D_IN_PROJ = 3 * D_CONV + 3 * D_ATTN
N_PEER_HEADS = 8
PEER_TOPK = 16
N_KEYS = 128
N_EXPERTS = N_KEYS * N_KEYS
D_QUERY = 256
D_SUBKEY = D_QUERY // 2
PEER_TOKEN_BLOCK = 128
EPS = 1e-6

kernel_name = "hybrid_conv_stickbreak_peer"


def rmsnorm(x, g):
    xf = x.astype(jnp.float32)
    y = xf * lax.rsqrt(jnp.mean(xf * xf, axis=-1, keepdims=True) + EPS)
    return (y * g.astype(jnp.float32)).astype(x.dtype)


def short_gated_conv(gate_b, gate_c, conv_in, conv_w):
    t = conv_in.shape[1]
    u = gate_c * conv_in
    up = jnp.pad(u, ((0, 0), (CONV_K - 1, 0), (0, 0)))
    y = up[:, 0:t] * conv_w[0]
    for j in range(1, CONV_K):
        y = y + up[:, j:j + t] * conv_w[j]
    return gate_b * y


def stick_breaking_attention(q, k, v):
    b, nh, t, dh = q.shape
    t_pad = -(-t // Q_BLOCK) * Q_BLOCK
    pad = ((0, 0), (0, 0), (0, t_pad - t), (0, 0))
    q, k, v = jnp.pad(q, pad), jnp.pad(k, pad), jnp.pad(v, pad)
    scale = dh ** -0.5
    outs = []
    for blk in range(t_pad // Q_BLOCK):
        q_lo = blk * Q_BLOCK
        kv_len = q_lo + Q_BLOCK
        qb = q[:, :, q_lo:kv_len]
        kb = k[:, :, :kv_len]
        vb = v[:, :, :kv_len]
        z = jnp.einsum('bhqd,bhkd->bhqk', qb, kb).astype(jnp.float32) * scale
        q_pos = q_lo + jnp.arange(Q_BLOCK)[:, None]
        k_pos = jnp.arange(kv_len)[None, :]
        past = k_pos < q_pos
        log_not = jnp.where(past, jax.nn.log_sigmoid(-z), 0.0)
        rest = lax.cumsum(log_not, axis=3, reverse=True) - log_not
        w = jnp.where(past, jnp.exp(jax.nn.log_sigmoid(z) + rest), 0.0)
        outs.append(jnp.einsum('bhqk,bhkd->bhqd', w.astype(vb.dtype), vb))
    return jnp.concatenate(outs, axis=2)[:, :, :t]


def mixer_sublayer(h, norm_g, w_in, conv_w, q_norm_g, k_norm_g, out_norm_g, w_out):
    b, t, _ = h.shape
    xn = rmsnorm(h, norm_g)
    proj = xn @ w_in
    gate_b, gate_c, conv_in, q, k, v = jnp.split(
        proj, [D_CONV, 2 * D_CONV, 3 * D_CONV, 3 * D_CONV + D_ATTN, 3 * D_CONV + 2 * D_ATTN], axis=-1)
    conv_out = short_gated_conv(gate_b, gate_c, conv_in, conv_w)
    def heads(a):
        return a.reshape(b, t, N_ATTN_HEADS, ATTN_HEAD_DIM)
    qh = rmsnorm(heads(q), q_norm_g).transpose(0, 2, 1, 3)
    kh = rmsnorm(heads(k), k_norm_g).transpose(0, 2, 1, 3)
    vh = heads(v).transpose(0, 2, 1, 3)
    attn_out = stick_breaking_attention(qh, kh, vh).transpose(0, 2, 1, 3)
    groups = jnp.concatenate([conv_out.reshape(b, t, CONV_GROUPS, HEAD_DIM_OUT), attn_out], axis=2)
    groups = rmsnorm(groups, jnp.ones((HEAD_DIM_OUT,), jnp.float32))
    mixed = groups.reshape(b, t, D_MIX) * out_norm_g
    return mixed @ w_out


def peer_sublayer(h, norm_g, w_query, sub_keys, expert_u, expert_v):
    b, t, d = h.shape
    n = b * t
    n_pad = -(-n // PEER_TOKEN_BLOCK) * PEER_TOKEN_BLOCK
    tok = jnp.pad(rmsnorm(h, norm_g).reshape(n, d), ((0, n_pad - n), (0, 0)))
    qry = (tok @ w_query).reshape(n_pad, N_PEER_HEADS, 2, D_SUBKEY)
    scores = jnp.einsum('nhpc,pkc->nhpk', qry, sub_keys).astype(jnp.float32)
    s_top, i_top = lax.top_k(scores, PEER_TOPK)
    cand_score = s_top[:, :, 0, :, None] + s_top[:, :, 1, None, :]
    cand_idx = i_top[:, :, 0, :, None] * N_KEYS + i_top[:, :, 1, None, :]
    best, pos = lax.top_k(cand_score.reshape(n_pad, N_PEER_HEADS, PEER_TOPK * PEER_TOPK), PEER_TOPK)
    experts = jnp.take_along_axis(cand_idx.reshape(n_pad, N_PEER_HEADS, PEER_TOPK * PEER_TOPK), pos, axis=-1)
    gates = jax.nn.softmax(best, axis=-1).astype(tok.dtype)

    def block_fn(args):
        xb, eb, gb = args
        u = expert_u[eb]
        vv = expert_v[eb]
        act = jax.nn.gelu(jnp.einsum('thkd,td->thk', u, xb), approximate=False)
        return jnp.einsum('thk,thkd->td', gb * act, vv)

    nb = n_pad // PEER_TOKEN_BLOCK
    out = lax.map(block_fn, (tok.reshape(nb, PEER_TOKEN_BLOCK, d),
                             experts.reshape(nb, PEER_TOKEN_BLOCK, N_PEER_HEADS, PEER_TOPK),
                             gates.reshape(nb, PEER_TOKEN_BLOCK, N_PEER_HEADS, PEER_TOPK)))
    return out.reshape(n_pad, d)[:n].reshape(b, t, d)


def setup_inputs(seed: int = 0) -> dict:
    key = jax.random.key(seed)
    ks = jax.random.split(key, 16)
    f32 = jnp.float32

    def nrm(k, shape, scale):
        return jax.random.normal(k, shape, f32) * scale

    return {
        "x": nrm(ks[0], (BATCH, SEQ, D_MODEL), 1.0),
        "meta_tokens": nrm(ks[1], (N_META, D_MODEL), 1.0),
        "norm_mix_g": 1.0 + nrm(ks[2], (DEPTH, D_MODEL), 0.02),
        "w_in": nrm(ks[3], (DEPTH, D_MODEL, D_IN_PROJ), D_MODEL ** -0.5),
        "conv_w": nrm(ks[4], (DEPTH, CONV_K, D_CONV), CONV_K ** -0.5),
        "q_norm_g": 1.0 + nrm(ks[5], (DEPTH, ATTN_HEAD_DIM), 0.02),
        "k_norm_g": 1.0 + nrm(ks[6], (DEPTH, ATTN_HEAD_DIM), 0.02),
        "out_norm_g": 1.0 + nrm(ks[7], (DEPTH, D_MIX), 0.02),
        "w_out": nrm(ks[8], (DEPTH, D_MIX, D_MODEL), D_MIX ** -0.5),
        "norm_ffn_g": 1.0 + nrm(ks[9], (DEPTH, D_MODEL), 0.02),
        "w_query": nrm(ks[10], (DEPTH, D_MODEL, N_PEER_HEADS * D_QUERY), D_MODEL ** -0.5),
        "sub_keys": nrm(ks[11], (DEPTH, 2, N_KEYS, D_SUBKEY), D_SUBKEY ** -0.5),
        "expert_u": nrm(ks[12], (DEPTH, N_EXPERTS, D_MODEL), D_MODEL ** -0.5),
        "expert_v": nrm(ks[13], (DEPTH, N_EXPERTS, D_MODEL), N_PEER_HEADS ** -0.5),
    }


def reference(x, meta_tokens, norm_mix_g, w_in, conv_w, q_norm_g, k_norm_g, out_norm_g, w_out,
              norm_ffn_g, w_query, sub_keys, expert_u, expert_v):
    b = x.shape[0]
    meta = jnp.broadcast_to(meta_tokens.astype(x.dtype)[None], (b, N_META, D_MODEL))
    h = jnp.concatenate([meta, x], axis=1)
    for layer in range(DEPTH):
        h = h + mixer_sublayer(h, norm_mix_g[layer], w_in[layer], conv_w[layer], q_norm_g[layer],
                               k_norm_g[layer], out_norm_g[layer], w_out[layer])
        h = h + peer_sublayer(h, norm_ffn_g[layer], w_query[layer], sub_keys[layer],
                              expert_u[layer], expert_v[layer])
    return h[:, N_META:]
```

```python
import functools

import jax
import jax.numpy as jnp
from jax import lax
from jax.experimental import pallas as pl
from jax.experimental.pallas import tpu as pltpu

F32 = jnp.float32
BF16 = jnp.bfloat16
I32 = jnp.int32

LANES = 128
N_META = 16
CONV_K = 3
N_HEADS = 8
HEAD_DIM = 128
Q_BLOCK = 128
PEER_TOPK = 16
N_KEYS = 128
D_SUBKEY = 128
EPS = 1e-6

TOKEN_BLOCK = 128
EXPERT_TOKEN_BLOCK = 64
PAIRS = 512
N_CHUNKS = 4
WORDS_PER_EXPERT = 8
GATHER_STRIDE = PAIRS + 4
ITEMS_PER_FETCH = 8
GATHER_PIECES = 4
PACK_ROWS = 256
VMEM_LIMIT = 56 * 1024 * 1024


def _nt_dot(a, b):
    return lax.dot_general(a, b, (((1,), (1,)), ((), ())), preferred_element_type=F32)


def _rms(x, eps=EPS):
    return x * lax.rsqrt(jnp.mean(x * x, axis=-1, keepdims=True) + eps)


def _inproj_kernel(h_ref, g_ref, w_ref, o_ref):
    xn = (_rms(h_ref[...]) * g_ref[...]).astype(BF16)
    o_ref[...] = jnp.dot(xn, w_ref[...], preferred_element_type=F32).astype(o_ref.dtype)


def _inproj(h, g, w, tm, tn):
    n, d = h.shape
    dout = w.shape[1]
    return pl.pallas_call(
        _inproj_kernel,
        grid=(dout // tn, n // tm),
        in_specs=[
            pl.BlockSpec((tm, d), lambda j, i: (i, 0)),
            pl.BlockSpec((1, d), lambda j, i: (0, 0)),
            pl.BlockSpec((d, tn), lambda j, i: (0, j)),
        ],
        out_specs=pl.BlockSpec((tm, tn), lambda j, i: (i, j)),
        out_shape=jax.ShapeDtypeStruct((n, dout), BF16),
        compiler_params=pltpu.CompilerParams(
            dimension_semantics=("arbitrary", "arbitrary"), vmem_limit_bytes=VMEM_LIMIT),
        name="inproj",
    )(h, g, w)


def _conv_kernel(gb_ref, gc_ref, ci_ref, w_ref, og_ref, o_ref):
    u = gc_ref[0].astype(F32) * ci_ref[0].astype(F32)
    row = lax.broadcasted_iota(I32, u.shape, 0)
    u1 = jnp.where(row >= 1, pltpu.roll(u, 1, axis=0), 0.0)
    u2 = jnp.where(row >= 2, pltpu.roll(u, 2, axis=0), 0.0)
    w = w_ref[...]
    y = u2 * w[0:1, :] + u1 * w[1:2, :] + u * w[2:3, :]
    y = gb_ref[0].astype(F32) * y
    o_ref[0] = (_rms(y) * og_ref[...]).astype(o_ref.dtype)


def _conv_group(proj3, conv_w, out_g):
    b, t, _ = proj3.shape
    blk = lambda off: pl.BlockSpec((1, t, LANES), lambda bi, g: (bi, 0, off + g))
    return pl.pallas_call(
        _conv_kernel,
        grid=(b, N_HEADS),
        in_specs=[
            blk(0), blk(N_HEADS), blk(2 * N_HEADS),
            pl.BlockSpec((CONV_K, LANES), lambda bi, g: (0, g)),
            pl.BlockSpec((1, LANES), lambda bi, g: (0, g)),
        ],
        out_specs=pl.BlockSpec((1, t, LANES), lambda bi, g: (bi, 0, g)),
        out_shape=jax.ShapeDtypeStruct((b, t, N_HEADS * LANES), BF16),
        compiler_params=pltpu.CompilerParams(
            dimension_semantics=("arbitrary", "arbitrary"), vmem_limit_bytes=VMEM_LIMIT),
        name="conv_group",
    )(proj3, proj3, proj3, conv_w, out_g)


def _attn_kernel(q_ref, k_ref, v_ref, gq_ref, gk_ref, og_ref, o_ref, qs, ks, vs, osc, acc, later, *, t, tp, nh):
    nblk = tp // Q_BLOCK
    pad = tp - t
    hsl = [slice(h * HEAD_DIM, (h + 1) * HEAD_DIM) for h in range(nh)]

    def nrm(x, g):
        return (_rms(x.astype(F32)) * g).astype(BF16)

    for h in range(nh):
        qs[pl.ds(0, t), hsl[h]] = nrm(q_ref[0, :, hsl[h]], gq_ref[...])
        ks[pl.ds(0, t), hsl[h]] = nrm(k_ref[0, :, hsl[h]], gk_ref[...])
    vs[pl.ds(0, t), :] = v_ref[0]
    if pad:
        zpad = jnp.zeros((pad, nh * HEAD_DIM), BF16)
        qs[pl.ds(t, pad), :] = zpad
        ks[pl.ds(t, pad), :] = zpad
        vs[pl.ds(t, pad), :] = zpad

    scale = HEAD_DIM ** -0.5
    r_i = lax.broadcasted_iota(I32, (Q_BLOCK, Q_BLOCK), 0)
    c_i = lax.broadcasted_iota(I32, (Q_BLOCK, Q_BLOCK), 1)
    past = jnp.concatenate([c_i < r_i] * nh, axis=0)
    cum_w = jnp.concatenate([(r_i > c_i).astype(BF16), jnp.ones((Q_BLOCK, Q_BLOCK), BF16)], axis=1)
    cum_w = jnp.concatenate([cum_w, cum_w], axis=0)

    def tiles(qoff, koff, diag):
        z = jnp.concatenate([_nt_dot(qs[pl.ds(qoff, Q_BLOCK), hsl[h]], ks[pl.ds(koff, Q_BLOCK), hsl[h]])
                             for h in range(nh)], axis=0) * scale
        sp = jnp.maximum(z, 0.0) + jnp.log(1.0 + jnp.exp(-jnp.abs(z)))
        log_not = -sp
        if diag:
            log_not = jnp.where(past, log_not, 0.0)
        hi = log_not.astype(BF16)
        lo = (log_not - hi.astype(F32)).astype(BF16)
        cs = jnp.dot(jnp.concatenate([hi, lo], axis=1), cum_w, preferred_element_type=F32)
        rest = cs[:, :Q_BLOCK]
        if not diag:
            rest = rest + later[...]
        w = jnp.exp((z - sp) + rest)
        if diag:
            w = jnp.where(past, w, 0.0)
        wb = w.astype(BF16)
        pv = jnp.concatenate([jnp.dot(wb[h * Q_BLOCK:(h + 1) * Q_BLOCK], vs[pl.ds(koff, Q_BLOCK), hsl[h]],
                                      preferred_element_type=F32) for h in range(nh)], axis=0)
        if diag:
            acc[...] = pv
            later[...] = cs[:, Q_BLOCK:]
        else:
            acc[...] += pv
            later[...] += cs[:, Q_BLOCK:]

    def q_body(qi, carry):
        qoff = pl.multiple_of(qi * Q_BLOCK, Q_BLOCK)
        tiles(qoff, qoff, True)

        def k_body(i, c):
            tiles(qoff, pl.multiple_of((qi - 1 - i) * Q_BLOCK, Q_BLOCK), False)
            return c

        lax.fori_loop(0, qi, k_body, 0)
        for h in range(nh):
            a = acc[h * Q_BLOCK:(h + 1) * Q_BLOCK, :]
            osc[pl.ds(qoff, Q_BLOCK), hsl[h]] = (_rms(a) * og_ref[:, hsl[h]]).astype(osc.dtype)
        return carry

    lax.fori_loop(0, nblk, q_body, 0)
    o_ref[0] = osc[pl.ds(0, t), :]


def _attn_group(proj3, gq, gk, out_g):
    b, t, _ = proj3.shape
    tp = -(-t // Q_BLOCK) * Q_BLOCK
    nh = N_HEADS
    w = nh * HEAD_DIM
    blk = lambda off: pl.BlockSpec((1, t, w), lambda bi: (bi, 0, off))
    return pl.pallas_call(
        functools.partial(_attn_kernel, t=t, tp=tp, nh=nh),
        grid=(b,),
        in_specs=[
            blk(3), blk(4), blk(5),
            pl.BlockSpec((1, HEAD_DIM), lambda bi: (0, 0)),
            pl.BlockSpec((1, HEAD_DIM), lambda bi: (0, 0)),
            pl.BlockSpec((1, w), lambda bi: (0, 1)),
        ],
        out_specs=pl.BlockSpec((1, t, w), lambda bi: (bi, 0, 0)),
        out_shape=jax.ShapeDtypeStruct((b, t, w), BF16),
        scratch_shapes=[pltpu.VMEM((tp, w), BF16)] * 4 + [pltpu.VMEM((nh * Q_BLOCK, Q_BLOCK), F32)] * 2,
        compiler_params=pltpu.CompilerParams(
            dimension_semantics=("arbitrary",), vmem_limit_bytes=VMEM_LIMIT),
        name="attn_group",
    )(proj3, proj3, proj3, gq, gk, out_g)


def _outproj_kernel(cm_ref, am_ref, wa_ref, wb_ref, h_ref, o_ref):
    o_ref[...] = (h_ref[...]
                  + jnp.dot(cm_ref[...], wa_ref[...], preferred_element_type=F32)
                  + jnp.dot(am_ref[...], wb_ref[...], preferred_element_type=F32))


def _outproj(cm, am, w, h, tm, tn):
    n, d = h.shape
    kh = cm.shape[1]
    return pl.pallas_call(
        _outproj_kernel,
        grid=(d // tn, n // tm),
        in_specs=[
            pl.BlockSpec((tm, kh), lambda j, i: (i, 0)),
            pl.BlockSpec((tm, kh), lambda j, i: (i, 0)),
            pl.BlockSpec((kh, tn), lambda j, i: (0, j)),
            pl.BlockSpec((kh, tn), lambda j, i: (1, j)),
            pl.BlockSpec((tm, tn), lambda j, i: (i, j)),
        ],
        out_specs=pl.BlockSpec((tm, tn), lambda j, i: (i, j)),
        out_shape=jax.ShapeDtypeStruct((n, d), F32),
        compiler_params=pltpu.CompilerParams(
            dimension_semantics=("arbitrary", "arbitrary"), vmem_limit_bytes=VMEM_LIMIT),
        name="outproj",
    )(cm, am, w, w, h)


def _query_kernel(h_ref, g_ref, w_ref, tok_ref, q_ref):
    xn = (_rms(h_ref[...]) * g_ref[...]).astype(BF16)
    tok_ref[...] = xn
    q_ref[...] = jnp.dot(xn, w_ref[...], preferred_element_type=F32).astype(q_ref.dtype)


def _query(h, g, w, tm, tn):
    n, d = h.shape
    dq = w.shape[1]
    return pl.pallas_call(
        _query_kernel,
        grid=(n // tm, dq // tn),
        in_specs=[
            pl.BlockSpec((tm, d), lambda i, j: (i, 0)),
            pl.BlockSpec((1, d), lambda i, j: (0, 0)),
            pl.BlockSpec((d, tn), lambda i, j: (0, j)),
        ],
        out_specs=[
            pl.BlockSpec((tm, d), lambda i, j: (i, 0)),
            pl.BlockSpec((tm, tn), lambda i, j: (i, j)),
        ],
        out_shape=[jax.ShapeDtypeStruct((n, d), BF16), jax.ShapeDtypeStruct((n, dq), BF16)],
        compiler_params=pltpu.CompilerParams(
            dimension_semantics=("arbitrary", "arbitrary"), vmem_limit_bytes=VMEM_LIMIT),
        name="query",
    )(h, g, w)


def _topk_rows(s, row, n_rows, k):
    vals, idxs = [], []
    for _ in range(k):
        m = jnp.max(s, axis=0, keepdims=True)
        idx = jnp.min(jnp.where(s == m, row, float(n_rows)), axis=0, keepdims=True)
        vals.append(m)
        idxs.append(idx)
        s = jnp.where(row == idx, -jnp.inf, s)
    return vals, idxs


def _stack_rows(rows, row16):
    out = jnp.broadcast_to(rows[0], row16.shape)
    for i in range(1, len(rows)):
        out = jnp.where(row16 == float(i), rows[i], out)
    return out


def _route_kernel(q_ref, sk_ref, e_ref, g_ref, es, gs):
    tm = q_ref.shape[0]
    k = PEER_TOPK
    row_keys = lax.broadcasted_iota(I32, (N_KEYS, tm), 0).astype(F32)
    row16 = lax.broadcasted_iota(I32, (k, tm), 0).astype(F32)
    for h in range(N_HEADS):
        tops = []
        for p in range(2):
            c0 = (2 * h + p) * D_SUBKEY
            s = _nt_dot(sk_ref[p], q_ref[:, c0:c0 + D_SUBKEY])
            tops.append(_topk_rows(s, row_keys, N_KEYS, k))
        (v1, i1), (v2, i2) = tops
        v2s = _stack_rows(v2, row16)
        i2s = _stack_rows(i2, row16)
        half = k // 2
        v1_hi = _stack_rows(v1[half:], row16[:half])
        i1_hi = _stack_rows(i1[half:], row16[:half])
        cand = jnp.concatenate([v1[0] + v2s] + [v1[a] + v2s[:half] for a in range(1, half)]
                               + [v1_hi + v2[0]], axis=0)
        cidx = jnp.concatenate([i1[0] * float(N_KEYS) + i2s]
                               + [i1[a] * float(N_KEYS) + i2s[:half] for a in range(1, half)]
                               + [i1_hi * float(N_KEYS) + i2[0]], axis=0)
        row_cand = jnp.concatenate([row16] + [row16[:half] + float(a * k) for a in range(1, half)]
                                   + [(row16[:half] + float(half)) * float(k)], axis=0)
        best, exps = [], []
        for _ in range(k):
            m = jnp.max(cand, axis=0, keepdims=True)
            pos = jnp.min(jnp.where(cand == m, row_cand, float(k * k)), axis=0, keepdims=True)
            sel = row_cand == pos
            exps.append(jnp.sum(jnp.where(sel, cidx, 0.0), axis=0, keepdims=True))
            best.append(m)
            cand = jnp.where(sel, -jnp.inf, cand)
        ex = [jnp.exp(bv - best[0]) for bv in best]
        den = ex[0]
        for e in ex[1:]:
            den = den + e
        es[h * k:(h + 1) * k, :] = _stack_rows(exps, row16)
        gs[h * k:(h + 1) * k, :] = _stack_rows([e / den for e in ex], row16)
    e_ref[...] = es[...].T.astype(I32)
    g_ref[...] = gs[...].T


def _route(qry, sub_keys):
    n, dq = qry.shape
    tm = TOKEN_BLOCK
    slots = N_HEADS * PEER_TOPK
    return pl.pallas_call(
        _route_kernel,
        grid=(n // tm,),
        in_specs=[
            pl.BlockSpec((tm, dq), lambda i: (i, 0)),
            pl.BlockSpec((2, N_KEYS, D_SUBKEY), lambda i: (0, 0, 0)),
        ],
        out_specs=[
            pl.BlockSpec((tm, slots), lambda i: (i, 0)),
            pl.BlockSpec((tm, slots), lambda i: (i, 0)),
        ],
        out_shape=[jax.ShapeDtypeStruct((n, slots), I32), jax.ShapeDtypeStruct((n, slots), F32)],
        scratch_shapes=[pltpu.VMEM((slots, tm), F32)] * 2,
        compiler_params=pltpu.CompilerParams(
            dimension_semantics=("arbitrary",), vmem_limit_bytes=VMEM_LIMIT),
        name="route",
    )(qry, sub_keys)


def _expert_kernel(ic_ref, itx_ref, ioc_ref, ito_ref, ifl_ref, eoff_ref, eoff_nx_ref, t2_ref, g2_ref, x_ref, uv_ref,
                   o_ref, tu0, tu1, tu2, tv0, tv1, tv2, rs0, rs1, rs2):
    i = pl.program_id(0)
    flags = ifl_ref[i]
    nj = WORDS_PER_EXPERT
    s_rows = GATHER_STRIDE
    tb = x_ref.shape[0]
    row = i % ITEMS_PER_FETCH
    row_nx = (i + 1) % ITEMS_PER_FETCH
    row_pv = (i + ITEMS_PER_FETCH - 1) % ITEMS_PER_FETCH
    tus, tvs, rss = (tu0, tu1, tu2), (tv0, tv1, tv2), (rs0, rs1, rs2)

    def gather(eref, r, tu, tv, p0=0, p1=PAIRS):
        for p in range(p0, p1):
            off = pl.multiple_of(eref[0, r, p], 2 * nj)
            tu[pl.ds(p, nj, stride=s_rows), :] = uv_ref[pl.ds(off, nj), :]
            tv[pl.ds(p, nj, stride=s_rows), :] = uv_ref[pl.ds(off + nj, nj), :]

    def table(tile):
        words = jnp.concatenate([tile[j * s_rows:j * s_rows + PAIRS, :] for j in range(nj)], axis=1)
        return pltpu.bitcast(words, BF16)

    def x_halves():
        x = x_ref[...]
        x_lo = jnp.concatenate([x[:, 2 * j * LANES:(2 * j + 1) * LANES] for j in range(nj)], axis=1)
        x_hi = jnp.concatenate([x[:, (2 * j + 1) * LANES:(2 * j + 2) * LANES] for j in range(nj)], axis=1)
        return jnp.concatenate([x_lo, x_hi], axis=0)

    def coefficients(rs):
        lane = lax.broadcasted_iota(I32, (tb, 2 * PAIRS), 1)
        even = (lane & 1) == 0
        rsel = jnp.where(even, rs[:tb, :], rs[tb:, :])
        mask = lax.broadcasted_iota(I32, (tb, 2 * PAIRS), 0) == t2_ref[0, pl.ds(row_pv, 1), :]
        s2 = jnp.sum(jnp.where(mask, rsel, 0.0), axis=0, keepdims=True)
        s2 = jnp.broadcast_to(s2, (8, 2 * PAIRS))
        even8 = even[:8]
        s = s2 + jnp.where(even8, pltpu.roll(s2, 2 * PAIRS - 1, axis=1), pltpu.roll(s2, 1, axis=1))
        act = 0.5 * s * (1.0 + lax.erf(s * (2.0 ** -0.5)))
        coef = g2_ref[0, pl.ds(row_pv, 1), :] * act[0:1]
        cm = jnp.where(mask, coef, 0.0)
        return jnp.concatenate([jnp.where(even, cm, 0.0), jnp.where(even, 0.0, cm)], axis=0).astype(BF16)

    def accumulate(o):
        for j in range(nj):
            c0 = 2 * j * LANES
            o_ref[0, :, c0:c0 + LANES] += o[:tb, j * LANES:(j + 1) * LANES]
            o_ref[0, :, c0 + LANES:c0 + 2 * LANES] += o[tb:, j * LANES:(j + 1) * LANES]

    def main(k):
        nx, pv = (k + 1) % 3, (k + 2) % 3
        q = PAIRS // GATHER_PIECES
        g = lambda a: gather(eoff_nx_ref, row_nx, tus[nx], tvs[nx], a * q, (a + 1) * q)
        g(0)
        rss[k][...] = _nt_dot(x_halves(), table(tus[k]))
        g(1)
        cf = coefficients(rss[pv])
        g(2)
        o = jnp.dot(cf, table(tvs[pv]), preferred_element_type=F32)
        g(3)
        accumulate(o)

    phase = i % 3
    own = (flags & 1) != 0

    @pl.when(i == 0)
    def _():
        rs2[...] = jnp.zeros(rs2.shape, rs2.dtype)
        tv2[...] = jnp.zeros(tv2.shape, tv2.dtype)

    @pl.when((flags & 2) != 0)
    def _():
        o_ref[...] = jnp.zeros(o_ref.shape, o_ref.dtype)

    for k in range(3):
        @pl.when(own & (phase == k))
        def _(k=k):
            gather(eoff_ref, row, tus[k], tvs[k])

    for k in range(3):
        @pl.when(phase == k)
        def _(k=k):
            main(k)


def _experts(ic, itb, ifl, eoff, t2, g2, tok, uv_pk):
    n, d = tok.shape
    n_items = ic.shape[0]
    chunk_rows = uv_pk.shape[0] // N_CHUNKS
    tb = EXPERT_TOKEN_BLOCK
    nf = ITEMS_PER_FETCH
    true1 = jnp.ones((1,), bool)
    ic_s = jnp.concatenate([ic, ic[-1:]])
    itx_s = jnp.concatenate([itb, itb[-1:]])
    ioc_s = jnp.concatenate([ic[:1], ic])
    ito_s = jnp.concatenate([itb[:1], itb])
    own_u = jnp.concatenate([true1, ic_s[1:] != ic_s[:-1]])
    first_o = jnp.concatenate([true1, (ifl & 2) != 0])
    fl_s = own_u.astype(I32) + 2 * first_o.astype(I32)
    n_rows = -(-(n_items + 2) // nf) * nf
    pad_rows = lambda a: jnp.concatenate(
        [a, jnp.broadcast_to(a[-1:], (n_rows - n_items,) + a.shape[1:])], axis=0).reshape(n_rows // nf, nf, -1)
    eoff3, t23, g23 = pad_rows(eoff), pad_rows(t2), pad_rows(g2)
    blk_cur = lambda i, ic, itx, ioc, ito, fl: (i // nf, 0, 0)
    blk_nxt = lambda i, ic, itx, ioc, ito, fl: ((i + 1) // nf, 0, 0)
    blk_prv = lambda i, ic, itx, ioc, ito, fl: (jnp.maximum(i - 1, 0) // nf, 0, 0)
    grid_spec = pltpu.PrefetchScalarGridSpec(
        num_scalar_prefetch=5,
        grid=(n_items + 1,),
        in_specs=[
            pl.BlockSpec((1, nf, PAIRS), blk_cur, memory_space=pltpu.SMEM),
            pl.BlockSpec((1, nf, PAIRS), blk_nxt, memory_space=pltpu.SMEM),
            pl.BlockSpec((1, nf, 2 * PAIRS), blk_prv),
            pl.BlockSpec((1, nf, 2 * PAIRS), blk_prv),
            pl.BlockSpec((tb, d), lambda i, ic, itx, ioc, ito, fl: (itx[i], 0)),
            pl.BlockSpec((chunk_rows, LANES), lambda i, ic, itx, ioc, ito, fl: (ic[i], 0),
                         pipeline_mode=pl.Buffered(1)),
        ],
        out_specs=pl.BlockSpec((1, tb, d), lambda i, ic, itx, ioc, ito, fl: (ioc[i], ito[i], 0)),
        scratch_shapes=[pltpu.VMEM((WORDS_PER_EXPERT * GATHER_STRIDE, LANES), I32)] * 6
        + [pltpu.VMEM((2 * tb, 2 * PAIRS), F32)] * 3,
    )
    return pl.pallas_call(
        _expert_kernel,
        grid_spec=grid_spec,
        out_shape=jax.ShapeDtypeStruct((N_CHUNKS, n, d), F32),
        compiler_params=pltpu.CompilerParams(
            dimension_semantics=("arbitrary",), vmem_limit_bytes=VMEM_LIMIT),
        name="experts",
    )(ic_s, itx_s, ioc_s, ito_s, fl_s, eoff3, eoff3, t23, g23, tok, uv_pk)


def _combine_kernel(pres_ref, h_ref, p_ref, o_ref, *, ntb, nsb):
    i = pl.program_id(0)
    tb = EXPERT_TOKEN_BLOCK
    for sb in range(nsb):
        rows = slice(sb * tb, (sb + 1) * tb)
        acc = h_ref[rows, :]
        for c in range(N_CHUNKS):
            have = pres_ref[c * ntb + i * nsb + sb] != 0
            acc = acc + jnp.where(have, p_ref[c, rows, :], 0.0)
        o_ref[rows, :] = acc


def _combine(h1, partial, present):
    n, d = h1.shape
    ntb = n // EXPERT_TOKEN_BLOCK
    nsb = next(c for c in (3, 2, 1) if ntb % c == 0)
    tm = nsb * EXPERT_TOKEN_BLOCK
    grid_spec = pltpu.PrefetchScalarGridSpec(
        num_scalar_prefetch=1,
        grid=(n // tm,),
        in_specs=[
            pl.BlockSpec((tm, d), lambda i, pres: (i, 0)),
            pl.BlockSpec((N_CHUNKS, tm, d), lambda i, pres: (0, i, 0)),
        ],
        out_specs=pl.BlockSpec((tm, d), lambda i, pres: (i, 0)),
    )
    return pl.pallas_call(
        functools.partial(_combine_kernel, ntb=ntb, nsb=nsb),
        grid_spec=grid_spec,
        out_shape=jax.ShapeDtypeStruct((n, d), F32),
        compiler_params=pltpu.CompilerParams(
            dimension_semantics=("arbitrary",), vmem_limit_bytes=VMEM_LIMIT),
        name="combine",
    )(present.reshape(-1).astype(I32), h1, partial)


def _pack_kernel(u_ref, v_ref, o_ref):
    r = u_ref.shape[0]
    nj = WORDS_PER_EXPERT
    for t, ref in enumerate((u_ref, v_ref)):
        for j in range(nj):
            lo = ref[:, 2 * j * LANES:(2 * j + 1) * LANES]
            hi = ref[:, (2 * j + 1) * LANES:(2 * j + 2) * LANES]
            words = pltpu.pack_elementwise([lo, hi], packed_dtype=BF16)
            o_ref[pl.ds(t * nj + j, r, stride=2 * nj), :] = pltpu.bitcast(words, I32)


def _pack_tables(u, v):
    e, d = u.shape
    assert d == 2 * LANES * WORDS_PER_EXPERT
    r = PACK_ROWS
    return pl.pallas_call(
        _pack_kernel,
        grid=(e // r,),
        in_specs=[pl.BlockSpec((r, d), lambda i: (i, 0)), pl.BlockSpec((r, d), lambda i: (i, 0))],
        out_specs=pl.BlockSpec((r * 2 * WORDS_PER_EXPERT, LANES), lambda i: (i, 0)),
        out_shape=jax.ShapeDtypeStruct((e * 2 * WORDS_PER_EXPERT, LANES), I32),
        compiler_params=pltpu.CompilerParams(
            dimension_semantics=("arbitrary",), vmem_limit_bytes=VMEM_LIMIT),
        name="pack_tables",
    )(u, v)


def _bucket_pairs(e_t, g_t, n_experts):
    n, slots = e_t.shape
    tblk = EXPERT_TOKEN_BLOCK
    ntb = n // tblk
    per_blk = slots * tblk
    nwin = per_blk // PAIRS
    chunk = n_experts // N_CHUNKS
    blk = lambda a: a.reshape(ntb, per_blk)
    key = blk(e_t) * per_blk + jnp.arange(per_blk, dtype=I32)[None, :]
    key_s, g_s = lax.sort((key, blk(g_t)), dimension=1, num_keys=1, is_stable=False)
    e_s = (key_s // per_blk).reshape(ntb * nwin, PAIRS)
    t_s = ((key_s % per_blk) // slots).reshape(ntb * nwin, PAIRS)
    g_s = g_s.reshape(ntb * nwin, PAIRS)
    cs = jnp.arange(N_CHUNKS, dtype=I32)[:, None, None]
    flag = jnp.any((e_s // chunk)[None] == cs, axis=2)
    n_items = ntb * (nwin + N_CHUNKS - 1)
    ids = jnp.nonzero(flag.reshape(-1), size=n_items, fill_value=-1)[0].astype(I32)
    valid = ids >= 0
    n_valid = jnp.sum(valid.astype(I32))
    ids = jnp.where(valid, ids, ids[jnp.maximum(n_valid - 1, 0)])
    ic = ids // (ntb * nwin)
    iw = ids % (ntb * nwin)
    itb = iw // nwin
    run_id = ic * ntb + itb
    first = jnp.concatenate([jnp.ones((1,), bool), run_id[1:] != run_id[:-1]]) & valid
    ifl = valid.astype(I32) + 2 * first.astype(I32)
    rows = jnp.stack([e_s, t_s, lax.bitcast_convert_type(g_s, I32)], axis=1)[iw]
    e_i, t_i, g_i = rows[:, 0], rows[:, 1], lax.bitcast_convert_type(rows[:, 2], F32)
    in_chunk = (e_i // chunk) == ic[:, None]
    own_e = jnp.max(jnp.where(in_chunk, e_i, -1), axis=1, keepdims=True)
    eoff = (jnp.where(in_chunk, e_i, own_e) - ic[:, None] * chunk) * (2 * WORDS_PER_EXPERT)
    g_i = jnp.where(in_chunk & valid[:, None], g_i, 0.0)
    dup = lambda a: jnp.repeat(a, 2, axis=1)
    present = jnp.any(flag.reshape(N_CHUNKS, ntb, nwin), axis=2)
    return (ic, itb, ifl, eoff, dup(t_i), dup(g_i)), present


def _row_tile(n, candidates):
    for c in candidates:
        if n % c == 0:
            return c
    raise ValueError(f"no row tile for {n}")


def kernel(x, meta_tokens, norm_mix_g, w_in, conv_w, q_norm_g, k_norm_g, out_norm_g, w_out, norm_ffn_g,
           w_query, sub_keys, expert_u, expert_v):
    b, seq, d = x.shape
    t = seq + N_META
    n = b * t
    depth = w_in.shape[0]
    n_experts = expert_u.shape[1]
    assert n % TOKEN_BLOCK == 0 and d == 2 * LANES * WORDS_PER_EXPERT
    tm = _row_tile(n, (688, 512, 256, 128))

    meta = jnp.broadcast_to(meta_tokens.astype(x.dtype)[None], (b, N_META, d))
    h = jnp.concatenate([meta, x], axis=1).reshape(n, d)
    for layer in range(depth):
        proj = _inproj(h, norm_mix_g[layer][None], w_in[layer].astype(BF16), tm, 1024)
        proj3 = proj.reshape(b, t, proj.shape[1])
        og = out_norm_g[layer][None]
        cm = _conv_group(proj3, conv_w[layer], og).reshape(n, -1)
        am = _attn_group(proj3, q_norm_g[layer][None], k_norm_g[layer][None], og).reshape(n, -1)
        h = _outproj(cm, am, w_out[layer].astype(BF16), h, tm, 1024)
        tok, qry = _query(h, norm_ffn_g[layer][None], w_query[layer].astype(BF16), tm, w_query.shape[2])
        e_t, g_t = _route(qry, sub_keys[layer].astype(BF16))
        items, present = _bucket_pairs(e_t, g_t, n_experts)
        partial = _experts(*items, tok, _pack_tables(expert_u[layer], expert_v[layer]))
        h = _combine(h, partial, present)
    return h.reshape(b, t, d)[:, N_META:]
```

```python
import functools

import jax
import jax.numpy as jnp
from jax import lax
from jax.experimental import pallas as pl
from jax.experimental.pallas import tpu as pltpu

F32 = jnp.float32
BF16 = jnp.bfloat16
I32 = jnp.int32

LANES = 128
N_META = 16
CONV_K = 3
N_HEADS = 8
HEAD_DIM = 128
Q_BLOCK = 128
PEER_TOPK = 16
N_KEYS = 128
D_SUBKEY = 128
EPS = 1e-6

TOKEN_BLOCK = 128
EXPERT_TOKEN_BLOCK = 64
PAIRS = 512
N_CHUNKS = 4
WORDS_PER_EXPERT = 8
GATHER_STRIDE = PAIRS + 4
ITEMS_PER_FETCH = 8
GATHER_PIECES = 4
PACK_ROWS = 256
VMEM_LIMIT = 56 * 1024 * 1024


def _nt_dot(a, b):
    return lax.dot_general(a, b, (((1,), (1,)), ((), ())), preferred_element_type=F32)


def _rms(x, eps=EPS):
    return x * lax.rsqrt(jnp.mean(x * x, axis=-1, keepdims=True) + eps)


def _inproj_kernel(h_ref, g_ref, w_ref, o_ref):
    xn = (_rms(h_ref[...]) * g_ref[...]).astype(BF16)
    o_ref[...] = jnp.dot(xn, w_ref[...], preferred_element_type=F32).astype(o_ref.dtype)


def _inproj(h, g, w, tm, tn):
    n, d = h.shape
    dout = w.shape[1]
    return pl.pallas_call(
        _inproj_kernel,
        grid=(dout // tn, n // tm),
        in_specs=[
            pl.BlockSpec((tm, d), lambda j, i: (i, 0)),
            pl.BlockSpec((1, d), lambda j, i: (0, 0)),
            pl.BlockSpec((d, tn), lambda j, i: (0, j)),
        ],
        out_specs=pl.BlockSpec((tm, tn), lambda j, i: (i, j)),
        out_shape=jax.ShapeDtypeStruct((n, dout), BF16),
        compiler_params=pltpu.CompilerParams(
            dimension_semantics=("arbitrary", "arbitrary"), vmem_limit_bytes=VMEM_LIMIT),
        name="inproj",
    )(h, g, w)


def _conv_kernel(gb_ref, gc_ref, ci_ref, w_ref, og_ref, o_ref):
    u = gc_ref[0].astype(F32) * ci_ref[0].astype(F32)
    row = lax.broadcasted_iota(I32, u.shape, 0)
    u1 = jnp.where(row >= 1, pltpu.roll(u, 1, axis=0), 0.0)
    u2 = jnp.where(row >= 2, pltpu.roll(u, 2, axis=0), 0.0)
    w = w_ref[...]
    y = u2 * w[0:1, :] + u1 * w[1:2, :] + u * w[2:3, :]
    y = gb_ref[0].astype(F32) * y
    o_ref[0] = (_rms(y) * og_ref[...]).astype(o_ref.dtype)


def _conv_group(proj3, conv_w, out_g):
    b, t, _ = proj3.shape
    blk = lambda off: pl.BlockSpec((1, t, LANES), lambda bi, g: (bi, 0, off + g))
    return pl.pallas_call(
        _conv_kernel,
        grid=(b, N_HEADS),
        in_specs=[
            blk(0), blk(N_HEADS), blk(2 * N_HEADS),
            pl.BlockSpec((CONV_K, LANES), lambda bi, g: (0, g)),
            pl.BlockSpec((1, LANES), lambda bi, g: (0, g)),
        ],
        out_specs=pl.BlockSpec((1, t, LANES), lambda bi, g: (bi, 0, g)),
        out_shape=jax.ShapeDtypeStruct((b, t, N_HEADS * LANES), BF16),
        compiler_params=pltpu.CompilerParams(
            dimension_semantics=("arbitrary", "arbitrary"), vmem_limit_bytes=VMEM_LIMIT),
        name="conv_group",
    )(proj3, proj3, proj3, conv_w, out_g)


def _attn_kernel(q_ref, k_ref, v_ref, gq_ref, gk_ref, og_ref, o_ref, qs, ks, vs, osc, acc, later, *, t, tp, nh):
    nblk = tp // Q_BLOCK
    pad = tp - t
    hsl = [slice(h * HEAD_DIM, (h + 1) * HEAD_DIM) for h in range(nh)]

    def nrm(x, g):
        return (_rms(x.astype(F32)) * g).astype(BF16)

    for h in range(nh):
        qs[pl.ds(0, t), hsl[h]] = nrm(q_ref[0, :, hsl[h]], gq_ref[...])
        ks[pl.ds(0, t), hsl[h]] = nrm(k_ref[0, :, hsl[h]], gk_ref[...])
    vs[pl.ds(0, t), :] = v_ref[0]
    if pad:
        zpad = jnp.zeros((pad, nh * HEAD_DIM), BF16)
        qs[pl.ds(t, pad), :] = zpad
        ks[pl.ds(t, pad), :] = zpad
        vs[pl.ds(t, pad), :] = zpad

    scale = HEAD_DIM ** -0.5
    r_i = lax.broadcasted_iota(I32, (Q_BLOCK, Q_BLOCK), 0)
    c_i = lax.broadcasted_iota(I32, (Q_BLOCK, Q_BLOCK), 1)
    past = jnp.concatenate([c_i < r_i] * nh, axis=0)
    cum_w = jnp.concatenate([(r_i > c_i).astype(BF16), jnp.ones((Q_BLOCK, Q_BLOCK), BF16)], axis=1)
    cum_w = jnp.concatenate([cum_w, cum_w], axis=0)

    def tiles(qoff, koff, diag):
        z = jnp.concatenate([_nt_dot(qs[pl.ds(qoff, Q_BLOCK), hsl[h]], ks[pl.ds(koff, Q_BLOCK), hsl[h]])
                             for h in range(nh)], axis=0) * scale
        sp = jnp.maximum(z, 0.0) + jnp.log(1.0 + jnp.exp(-jnp.abs(z)))
        log_not = -sp
        if diag:
            log_not = jnp.where(past, log_not, 0.0)
        hi = log_not.astype(BF16)
        lo = (log_not - hi.astype(F32)).astype(BF16)
        cs = jnp.dot(jnp.concatenate([hi, lo], axis=1), cum_w, preferred_element_type=F32)
        rest = cs[:, :Q_BLOCK]
        if not diag:
            rest = rest + later[...]
        w = jnp.exp((z - sp) + rest)
        if diag:
            w = jnp.where(past, w, 0.0)
        wb = w.astype(BF16)
        pv = jnp.concatenate([jnp.dot(wb[h * Q_BLOCK:(h + 1) * Q_BLOCK], vs[pl.ds(koff, Q_BLOCK), hsl[h]],
                                      preferred_element_type=F32) for h in range(nh)], axis=0)
        if diag:
            acc[...] = pv
            later[...] = cs[:, Q_BLOCK:]
        else:
            acc[...] += pv
            later[...] += cs[:, Q_BLOCK:]

    def q_body(qi, carry):
        qoff = pl.multiple_of(qi * Q_BLOCK, Q_BLOCK)
        tiles(qoff, qoff, True)

        def k_body(i, c):
            tiles(qoff, pl.multiple_of((qi - 1 - i) * Q_BLOCK, Q_BLOCK), False)
            return c

        lax.fori_loop(0, qi, k_body, 0)
        for h in range(nh):
            a = acc[h * Q_BLOCK:(h + 1) * Q_BLOCK, :]
            osc[pl.ds(qoff, Q_BLOCK), hsl[h]] = (_rms(a) * og_ref[:, hsl[h]]).astype(osc.dtype)
        return carry

    lax.fori_loop(0, nblk, q_body, 0)
    o_ref[0] = osc[pl.ds(0, t), :]


def _attn_group(proj3, gq, gk, out_g):
    b, t, _ = proj3.shape
    tp = -(-t // Q_BLOCK) * Q_BLOCK
    nh = N_HEADS
    w = nh * HEAD_DIM
    blk = lambda off: pl.BlockSpec((1, t, w), lambda bi: (bi, 0, off))
    return pl.pallas_call(
        functools.partial(_attn_kernel, t=t, tp=tp, nh=nh),
        grid=(b,),
        in_specs=[
            blk(3), blk(4), blk(5),
            pl.BlockSpec((1, HEAD_DIM), lambda bi: (0, 0)),
            pl.BlockSpec((1, HEAD_DIM), lambda bi: (0, 0)),
            pl.BlockSpec((1, w), lambda bi: (0, 1)),
        ],
        out_specs=pl.BlockSpec((1, t, w), lambda bi: (bi, 0, 0)),
        out_shape=jax.ShapeDtypeStruct((b, t, w), BF16),
        scratch_shapes=[pltpu.VMEM((tp, w), BF16)] * 4 + [pltpu.VMEM((nh * Q_BLOCK, Q_BLOCK), F32)] * 2,
        compiler_params=pltpu.CompilerParams(
            dimension_semantics=("arbitrary",), vmem_limit_bytes=VMEM_LIMIT),
        name="attn_group",
    )(proj3, proj3, proj3, gq, gk, out_g)


def _outproj_kernel(cm_ref, am_ref, wa_ref, wb_ref, h_ref, o_ref):
    o_ref[...] = (h_ref[...]
                  + jnp.dot(cm_ref[...], wa_ref[...], preferred_element_type=F32)
                  + jnp.dot(am_ref[...], wb_ref[...], preferred_element_type=F32))


def _outproj(cm, am, w, h, tm, tn):
    n, d = h.shape
    kh = cm.shape[1]
    return pl.pallas_call(
        _outproj_kernel,
        grid=(d // tn, n // tm),
        in_specs=[
            pl.BlockSpec((tm, kh), lambda j, i: (i, 0)),
            pl.BlockSpec((tm, kh), lambda j, i: (i, 0)),
            pl.BlockSpec((kh, tn), lambda j, i: (0, j)),
            pl.BlockSpec((kh, tn), lambda j, i: (1, j)),
            pl.BlockSpec((tm, tn), lambda j, i: (i, j)),
        ],
        out_specs=pl.BlockSpec((tm, tn), lambda j, i: (i, j)),
        out_shape=jax.ShapeDtypeStruct((n, d), F32),
        compiler_params=pltpu.CompilerParams(
            dimension_semantics=("arbitrary", "arbitrary"), vmem_limit_bytes=VMEM_LIMIT),
        name="outproj",
    )(cm, am, w, w, h)


def _query_kernel(h_ref, g_ref, w_ref, tok_ref, q_ref):
    xn = (_rms(h_ref[...]) * g_ref[...]).astype(BF16)
    tok_ref[...] = xn
    q_ref[...] = jnp.dot(xn, w_ref[...], preferred_element_type=F32).astype(q_ref.dtype)


def _query(h, g, w, tm, tn):
    n, d = h.shape
    dq = w.shape[1]
    return pl.pallas_call(
        _query_kernel,
        grid=(n // tm, dq // tn),
        in_specs=[
            pl.BlockSpec((tm, d), lambda i, j: (i, 0)),
            pl.BlockSpec((1, d), lambda i, j: (0, 0)),
            pl.BlockSpec((d, tn), lambda i, j: (0, j)),
        ],
        out_specs=[
            pl.BlockSpec((tm, d), lambda i, j: (i, 0)),
            pl.BlockSpec((tm, tn), lambda i, j: (i, j)),
        ],
        out_shape=[jax.ShapeDtypeStruct((n, d), BF16), jax.ShapeDtypeStruct((n, dq), BF16)],
        compiler_params=pltpu.CompilerParams(
            dimension_semantics=("arbitrary", "arbitrary"), vmem_limit_bytes=VMEM_LIMIT),
        name="query",
    )(h, g, w)


def _topk_rows(s, row, n_rows, k):
    vals, idxs = [], []
    for _ in range(k):
        m = jnp.max(s, axis=0, keepdims=True)
        idx = jnp.min(jnp.where(s == m, row, float(n_rows)), axis=0, keepdims=True)
        vals.append(m)
        idxs.append(idx)
        s = jnp.where(row == idx, -jnp.inf, s)
    return vals, idxs


def _stack_rows(rows, row16):
    out = jnp.broadcast_to(rows[0], row16.shape)
    for i in range(1, len(rows)):
        out = jnp.where(row16 == float(i), rows[i], out)
    return out


def _route_kernel(q_ref, sk_ref, e_ref, g_ref, es, gs):
    tm = q_ref.shape[0]
    k = PEER_TOPK
    row_keys = lax.broadcasted_iota(I32, (N_KEYS, tm), 0).astype(F32)
    row16 = lax.broadcasted_iota(I32, (k, tm), 0).astype(F32)
    for h in range(N_HEADS):
        tops = []
        for p in range(2):
            c0 = (2 * h + p) * D_SUBKEY
            s = _nt_dot(sk_ref[p], q_ref[:, c0:c0 + D_SUBKEY])
            tops.append(_topk_rows(s, row_keys, N_KEYS, k))
        (v1, i1), (v2, i2) = tops
        v2s = _stack_rows(v2, row16)
        i2s = _stack_rows(i2, row16)
        half = k // 2
        v1_hi = _stack_rows(v1[half:], row16[:half])
        i1_hi = _stack_rows(i1[half:], row16[:half])
        cand = jnp.concatenate([v1[0] + v2s] + [v1[a] + v2s[:half] for a in range(1, half)]
                               + [v1_hi + v2[0]], axis=0)
        cidx = jnp.concatenate([i1[0] * float(N_KEYS) + i2s]
                               + [i1[a] * float(N_KEYS) + i2s[:half] for a in range(1, half)]
                               + [i1_hi * float(N_KEYS) + i2[0]], axis=0)
        row_cand = jnp.concatenate([row16] + [row16[:half] + float(a * k) for a in range(1, half)]
                                   + [(row16[:half] + float(half)) * float(k)], axis=0)
        best, exps = [], []
        for _ in range(k):
            m = jnp.max(cand, axis=0, keepdims=True)
            pos = jnp.min(jnp.where(cand == m, row_cand, float(k * k)), axis=0, keepdims=True)
            sel = row_cand == pos
            exps.append(jnp.sum(jnp.where(sel, cidx, 0.0), axis=0, keepdims=True))
            best.append(m)
            cand = jnp.where(sel, -jnp.inf, cand)
        ex = [jnp.exp(bv - best[0]) for bv in best]
        den = ex[0]
        for e in ex[1:]:
            den = den + e
        es[h * k:(h + 1) * k, :] = _stack_rows(exps, row16)
        gs[h * k:(h + 1) * k, :] = _stack_rows([e / den for e in ex], row16)
    e_ref[...] = es[...].T.astype(I32)
    g_ref[...] = gs[...].T


def _route(qry, sub_keys):
    n, dq = qry.shape
    tm = TOKEN_BLOCK
    slots = N_HEADS * PEER_TOPK
    return pl.pallas_call(
        _route_kernel,
        grid=(n // tm,),
        in_specs=[
            pl.BlockSpec((tm, dq), lambda i: (i, 0)),
            pl.BlockSpec((2, N_KEYS, D_SUBKEY), lambda i: (0, 0, 0)),
        ],
        out_specs=[
            pl.BlockSpec((tm, slots), lambda i: (i, 0)),
            pl.BlockSpec((tm, slots), lambda i: (i, 0)),
        ],
        out_shape=[jax.ShapeDtypeStruct((n, slots), I32), jax.ShapeDtypeStruct((n, slots), F32)],
        scratch_shapes=[pltpu.VMEM((slots, tm), F32)] * 2,
        compiler_params=pltpu.CompilerParams(
            dimension_semantics=("arbitrary",), vmem_limit_bytes=VMEM_LIMIT),
        name="route",
    )(qry, sub_keys)


def _expert_kernel(ic_ref, itx_ref, ioc_ref, ito_ref, ifl_ref, eoff_ref, eoff_nx_ref, t2_ref, g2_ref, x_ref, uv_ref,
                   o_ref, tu0, tu1, tu2, tv0, tv1, tv2, rs0, rs1, rs2):
    i = pl.program_id(0)
    flags = ifl_ref[i]
    nj = WORDS_PER_EXPERT
    s_rows = GATHER_STRIDE
    tb = x_ref.shape[0]
    row = i % ITEMS_PER_FETCH
    row_nx = (i + 1) % ITEMS_PER_FETCH
    row_pv = (i + ITEMS_PER_FETCH - 1) % ITEMS_PER_FETCH
    tus, tvs, rss = (tu0, tu1, tu2), (tv0, tv1, tv2), (rs0, rs1, rs2)

    def gather(eref, r, tu, tv, p0=0, p1=PAIRS):
        for p in range(p0, p1):
            off = pl.multiple_of(eref[0, r, p], 2 * nj)
            tu[pl.ds(p, nj, stride=s_rows), :] = uv_ref[pl.ds(off, nj), :]
            tv[pl.ds(p, nj, stride=s_rows), :] = uv_ref[pl.ds(off + nj, nj), :]

    def table(tile):
        words = jnp.concatenate([tile[j * s_rows:j * s_rows + PAIRS, :] for j in range(nj)], axis=1)
        return pltpu.bitcast(words, BF16)

    def x_halves():
        x = x_ref[...]
        x_lo = jnp.concatenate([x[:, 2 * j * LANES:(2 * j + 1) * LANES] for j in range(nj)], axis=1)
        x_hi = jnp.concatenate([x[:, (2 * j + 1) * LANES:(2 * j + 2) * LANES] for j in range(nj)], axis=1)
        return jnp.concatenate([x_lo, x_hi], axis=0)

    def coefficients(rs):
        lane = lax.broadcasted_iota(I32, (tb, 2 * PAIRS), 1)
        even = (lane & 1) == 0
        rsel = jnp.where(even, rs[:tb, :], rs[tb:, :])
        mask = lax.broadcasted_iota(I32, (tb, 2 * PAIRS), 0) == t2_ref[0, pl.ds(row_pv, 1), :]
        s2 = jnp.sum(jnp.where(mask, rsel, 0.0), axis=0, keepdims=True)
        s2 = jnp.broadcast_to(s2, (8, 2 * PAIRS))
        even8 = even[:8]
        s = s2 + jnp.where(even8, pltpu.roll(s2, 2 * PAIRS - 1, axis=1), pltpu.roll(s2, 1, axis=1))
        act = 0.5 * s * (1.0 + lax.erf(s * (2.0 ** -0.5)))
        coef = g2_ref[0, pl.ds(row_pv, 1), :] * act[0:1]
        cm = jnp.where(mask, coef, 0.0)
        return jnp.concatenate([jnp.where(even, cm, 0.0), jnp.where(even, 0.0, cm)], axis=0).astype(BF16)

    def accumulate(o):
        for j in range(nj):
            c0 = 2 * j * LANES
            o_ref[0, :, c0:c0 + LANES] += o[:tb, j * LANES:(j + 1) * LANES]
            o_ref[0, :, c0 + LANES:c0 + 2 * LANES] += o[tb:, j * LANES:(j + 1) * LANES]

    def main(k):
        nx, pv = (k + 1) % 3, (k + 2) % 3
        q = PAIRS // GATHER_PIECES
        g = lambda a: gather(eoff_nx_ref, row_nx, tus[nx], tvs[nx], a * q, (a + 1) * q)
        g(0)
        rss[k][...] = _nt_dot(x_halves(), table(tus[k]))
        g(1)
        cf = coefficients(rss[pv])
        g(2)
        o = jnp.dot(cf, table(tvs[pv]), preferred_element_type=F32)
        g(3)
        accumulate(o)

    phase = i % 3
    own = (flags & 1) != 0

    @pl.when(i == 0)
    def _():
        rs2[...] = jnp.zeros(rs2.shape, rs2.dtype)
        tv2[...] = jnp.zeros(tv2.shape, tv2.dtype)

    @pl.when((flags & 2) != 0)
    def _():
        o_ref[...] = jnp.zeros(o_ref.shape, o_ref.dtype)

    for k in range(3):
        @pl.when(own & (phase == k))
        def _(k=k):
            gather(eoff_ref, row, tus[k], tvs[k])

    live = (flags & 4) != 0
    for k in range(3):
        @pl.when(live & (phase == k))
        def _(k=k):
            main(k)


def _experts(ic, itb, ifl, eoff, t2, g2, n_valid, tok, uv_pk):
    n, d = tok.shape
    n_items = ic.shape[0]
    chunk_rows = uv_pk.shape[0] // N_CHUNKS
    tb = EXPERT_TOKEN_BLOCK
    nf = ITEMS_PER_FETCH
    true1 = jnp.ones((1,), bool)
    ic_s = jnp.concatenate([ic, ic[-1:]])
    itx_s = jnp.concatenate([itb, itb[-1:]])
    ioc_s = jnp.concatenate([ic[:1], ic])
    ito_s = jnp.concatenate([itb[:1], itb])
    own_u = jnp.concatenate([true1, ic_s[1:] != ic_s[:-1]])
    first_o = jnp.concatenate([true1, (ifl & 2) != 0])
    live = jnp.arange(n_items + 1, dtype=I32) <= n_valid
    fl_s = own_u.astype(I32) + 2 * first_o.astype(I32) + 4 * live.astype(I32)
    n_rows = -(-(n_items + 2) // nf) * nf
    pad_rows = lambda a: jnp.concatenate(
        [a, jnp.broadcast_to(a[-1:], (n_rows - n_items,) + a.shape[1:])], axis=0).reshape(n_rows // nf, nf, -1)
    eoff3, t23, g23 = pad_rows(eoff), pad_rows(t2), pad_rows(g2)
    blk_cur = lambda i, ic, itx, ioc, ito, fl: (i // nf, 0, 0)
    blk_nxt = lambda i, ic, itx, ioc, ito, fl: ((i + 1) // nf, 0, 0)
    blk_prv = lambda i, ic, itx, ioc, ito, fl: (jnp.maximum(i - 1, 0) // nf, 0, 0)
    grid_spec = pltpu.PrefetchScalarGridSpec(
        num_scalar_prefetch=5,
        grid=(n_items + 1,),
        in_specs=[
            pl.BlockSpec((1, nf, PAIRS), blk_cur, memory_space=pltpu.SMEM),
            pl.BlockSpec((1, nf, PAIRS), blk_nxt, memory_space=pltpu.SMEM),
            pl.BlockSpec((1, nf, 2 * PAIRS), blk_prv),
            pl.BlockSpec((1, nf, 2 * PAIRS), blk_prv),
            pl.BlockSpec((tb, d), lambda i, ic, itx, ioc, ito, fl: (itx[i], 0)),
            pl.BlockSpec((chunk_rows, LANES), lambda i, ic, itx, ioc, ito, fl: (ic[i], 0),
                         pipeline_mode=pl.Buffered(1)),
        ],
        out_specs=pl.BlockSpec((1, tb, d), lambda i, ic, itx, ioc, ito, fl: (ioc[i], ito[i], 0)),
        scratch_shapes=[pltpu.VMEM((WORDS_PER_EXPERT * GATHER_STRIDE, LANES), I32)] * 6
        + [pltpu.VMEM((2 * tb, 2 * PAIRS), F32)] * 3,
    )
    return pl.pallas_call(
        _expert_kernel,
        grid_spec=grid_spec,
        out_shape=jax.ShapeDtypeStruct((N_CHUNKS, n, d), F32),
        compiler_params=pltpu.CompilerParams(
            dimension_semantics=("arbitrary",), vmem_limit_bytes=VMEM_LIMIT),
        name="experts",
    )(ic_s, itx_s, ioc_s, ito_s, fl_s, eoff3, eoff3, t23, g23, tok, uv_pk)


def _combine_kernel(pres_ref, h_ref, p_ref, o_ref, *, ntb, nsb):
    i = pl.program_id(0)
    tb = EXPERT_TOKEN_BLOCK
    for sb in range(nsb):
        rows = slice(sb * tb, (sb + 1) * tb)
        acc = h_ref[rows, :]
        for c in range(N_CHUNKS):
            have = pres_ref[c * ntb + i * nsb + sb] != 0
            acc = acc + jnp.where(have, p_ref[c, rows, :], 0.0)
        o_ref[rows, :] = acc


def _combine(h1, partial, present):
    n, d = h1.shape
    ntb = n // EXPERT_TOKEN_BLOCK
    nsb = next(c for c in (3, 2, 1) if ntb % c == 0)
    tm = nsb * EXPERT_TOKEN_BLOCK
    grid_spec = pltpu.PrefetchScalarGridSpec(
        num_scalar_prefetch=1,
        grid=(n // tm,),
        in_specs=[
            pl.BlockSpec((tm, d), lambda i, pres: (i, 0)),
            pl.BlockSpec((N_CHUNKS, tm, d), lambda i, pres: (0, i, 0)),
        ],
        out_specs=pl.BlockSpec((tm, d), lambda i, pres: (i, 0)),
    )
    return pl.pallas_call(
        functools.partial(_combine_kernel, ntb=ntb, nsb=nsb),
        grid_spec=grid_spec,
        out_shape=jax.ShapeDtypeStruct((n, d), F32),
        compiler_params=pltpu.CompilerParams(
            dimension_semantics=("arbitrary",), vmem_limit_bytes=VMEM_LIMIT),
        name="combine",
    )(present.reshape(-1).astype(I32), h1, partial)


def _pack_kernel(u_ref, v_ref, o_ref):
    r = u_ref.shape[0]
    nj = WORDS_PER_EXPERT
    for t, ref in enumerate((u_ref, v_ref)):
        for j in range(nj):
            lo = ref[:, 2 * j * LANES:(2 * j + 1) * LANES]
            hi = ref[:, (2 * j + 1) * LANES:(2 * j + 2) * LANES]
            words = pltpu.pack_elementwise([lo, hi], packed_dtype=BF16)
            o_ref[pl.ds(t * nj + j, r, stride=2 * nj), :] = pltpu.bitcast(words, I32)


def _pack_tables(u, v):
    e, d = u.shape
    assert d == 2 * LANES * WORDS_PER_EXPERT
    r = PACK_ROWS
    return pl.pallas_call(
        _pack_kernel,
        grid=(e // r,),
        in_specs=[pl.BlockSpec((r, d), lambda i: (i, 0)), pl.BlockSpec((r, d), lambda i: (i, 0))],
        out_specs=pl.BlockSpec((r * 2 * WORDS_PER_EXPERT, LANES), lambda i: (i, 0)),
        out_shape=jax.ShapeDtypeStruct((e * 2 * WORDS_PER_EXPERT, LANES), I32),
        compiler_params=pltpu.CompilerParams(
            dimension_semantics=("arbitrary",), vmem_limit_bytes=VMEM_LIMIT),
        name="pack_tables",
    )(u, v)


def _bucket_pairs(e_t, g_t, n_experts):
    n, slots = e_t.shape
    tblk = EXPERT_TOKEN_BLOCK
    ntb = n // tblk
    per_blk = slots * tblk
    nwin = per_blk // PAIRS
    chunk = n_experts // N_CHUNKS
    blk = lambda a: a.reshape(ntb, per_blk)
    key = blk(e_t) * per_blk + jnp.arange(per_blk, dtype=I32)[None, :]
    key_s, g_s = lax.sort((key, blk(g_t)), dimension=1, num_keys=1, is_stable=False)
    e_s = key_s // per_blk
    t_s = (key_s % per_blk) // slots
    cs = jnp.arange(N_CHUNKS, dtype=I32)
    cnt = jnp.sum(((e_s // chunk)[:, None, :] == cs[None, :, None]).astype(I32), axis=2)
    start = jnp.cumsum(cnt, axis=1) - cnt
    runs = -(-cnt // PAIRS)
    kmax = nwin + 1
    flag = jnp.arange(kmax, dtype=I32)[None, None, :] < runs.T[:, :, None]
    n_items = ntb * (nwin + N_CHUNKS)
    ids = jnp.nonzero(flag.reshape(-1), size=n_items, fill_value=-1)[0].astype(I32)
    valid = ids >= 0
    n_valid = jnp.sum(valid.astype(I32))
    ids = jnp.where(valid, ids, ids[jnp.maximum(n_valid - 1, 0)])
    ic = ids // (ntb * kmax)
    itb = (ids // kmax) % ntb
    off = start[itb, ic] + (ids % kmax) * PAIRS
    run_id = ic * ntb + itb
    first = jnp.concatenate([jnp.ones((1,), bool), run_id[1:] != run_id[:-1]]) & valid
    ifl = valid.astype(I32) + 2 * first.astype(I32)
    pad = lambda a, v: jnp.concatenate([a, jnp.full((ntb, PAIRS), v, I32)], axis=1)
    rows = jnp.stack([pad(e_s, n_experts), pad(t_s, 0), pad(lax.bitcast_convert_type(g_s, I32), 0)], axis=1)
    win = jax.vmap(lambda b, o: lax.dynamic_slice(rows, (b, 0, o), (1, 3, PAIRS))[0])(itb, off)
    e_i, t_i, g_i = win[:, 0], win[:, 1], lax.bitcast_convert_type(win[:, 2], F32)
    in_chunk = (e_i // chunk) == ic[:, None]
    own_e = jnp.max(jnp.where(in_chunk, e_i, -1), axis=1, keepdims=True)
    eoff = (jnp.where(in_chunk, e_i, own_e) - ic[:, None] * chunk) * (2 * WORDS_PER_EXPERT)
    g_i = jnp.where(in_chunk & valid[:, None], g_i, 0.0)
    dup = lambda a: jnp.repeat(a, 2, axis=1)
    return (ic, itb, ifl, eoff, dup(t_i), dup(g_i)), (cnt.T > 0), n_valid


def _row_tile(n, candidates):
    for c in candidates:
        if n % c == 0:
            return c
    raise ValueError(f"no row tile for {n}")


def kernel(x, meta_tokens, norm_mix_g, w_in, conv_w, q_norm_g, k_norm_g, out_norm_g, w_out, norm_ffn_g,
           w_query, sub_keys, expert_u, expert_v):
    b, seq, d = x.shape
    t = seq + N_META
    n = b * t
    depth = w_in.shape[0]
    n_experts = expert_u.shape[1]
    assert n % TOKEN_BLOCK == 0 and d == 2 * LANES * WORDS_PER_EXPERT
    tm = _row_tile(n, (688, 512, 256, 128))

    meta = jnp.broadcast_to(meta_tokens.astype(x.dtype)[None], (b, N_META, d))
    h = jnp.concatenate([meta, x], axis=1).reshape(n, d)
    for layer in range(depth):
        proj = _inproj(h, norm_mix_g[layer][None], w_in[layer].astype(BF16), tm, 1024)
        proj3 = proj.reshape(b, t, proj.shape[1])
        og = out_norm_g[layer][None]
        cm = _conv_group(proj3, conv_w[layer], og).reshape(n, -1)
        am = _attn_group(proj3, q_norm_g[layer][None], k_norm_g[layer][None], og).reshape(n, -1)
        h = _outproj(cm, am, w_out[layer].astype(BF16), h, tm, 1024)
        tok, qry = _query(h, norm_ffn_g[layer][None], w_query[layer].astype(BF16), tm, w_query.shape[2])
        e_t, g_t = _route(qry, sub_keys[layer].astype(BF16))
        items, present, n_valid = _bucket_pairs(e_t, g_t, n_experts)
        partial = _experts(*items, n_valid, tok, _pack_tables(expert_u[layer], expert_v[layer]))
        h = _combine(h, partial, present)
    return h.reshape(b, t, d)[:, N_META:]
```

```python
import functools

import jax
import jax.numpy as jnp
from jax import lax
from jax.experimental import pallas as pl
from jax.experimental.pallas import tpu as pltpu

F32 = jnp.float32
BF16 = jnp.bfloat16
I32 = jnp.int32

LANES = 128
N_META = 16
CONV_K = 3
N_HEADS = 8
HEAD_DIM = 128
Q_BLOCK = 128
PEER_TOPK = 16
N_KEYS = 128
D_SUBKEY = 128
EPS = 1e-6

TOKEN_BLOCK = 128
EXPERT_TOKEN_BLOCK = 64
PAIRS = 512
N_CHUNKS = 4
WORDS_PER_EXPERT = 8
GATHER_STRIDE = PAIRS + 4
ITEMS_PER_FETCH = 8
GATHER_PIECES = 4
PACK_ROWS = 256
VMEM_LIMIT = 56 * 1024 * 1024


def _nt_dot(a, b):
    return lax.dot_general(a, b, (((1,), (1,)), ((), ())), preferred_element_type=F32)


def _rms(x, eps=EPS):
    return x * lax.rsqrt(jnp.mean(x * x, axis=-1, keepdims=True) + eps)


def _inproj_kernel(h_ref, g_ref, w_ref, o_ref):
    xn = (_rms(h_ref[...]) * g_ref[...]).astype(BF16)
    o_ref[...] = jnp.dot(xn, w_ref[...], preferred_element_type=F32).astype(o_ref.dtype)


def _inproj(h, g, w, tm, tn):
    n, d = h.shape
    dout = w.shape[1]
    return pl.pallas_call(
        _inproj_kernel,
        grid=(dout // tn, n // tm),
        in_specs=[
            pl.BlockSpec((tm, d), lambda j, i: (i, 0)),
            pl.BlockSpec((1, d), lambda j, i: (0, 0)),
            pl.BlockSpec((d, tn), lambda j, i: (0, j)),
        ],
        out_specs=pl.BlockSpec((tm, tn), lambda j, i: (i, j)),
        out_shape=jax.ShapeDtypeStruct((n, dout), BF16),
        compiler_params=pltpu.CompilerParams(
            dimension_semantics=("arbitrary", "arbitrary"), vmem_limit_bytes=VMEM_LIMIT,
            allow_input_fusion=[True, False, True]),
        name="inproj",
    )(h, g, w)


def _conv_kernel(gb_ref, gc_ref, ci_ref, w_ref, og_ref, o_ref):
    u = gc_ref[0].astype(F32) * ci_ref[0].astype(F32)
    row = lax.broadcasted_iota(I32, u.shape, 0)
    u1 = jnp.where(row >= 1, pltpu.roll(u, 1, axis=0), 0.0)
    u2 = jnp.where(row >= 2, pltpu.roll(u, 2, axis=0), 0.0)
    w = w_ref[...]
    y = u2 * w[0:1, :] + u1 * w[1:2, :] + u * w[2:3, :]
    y = gb_ref[0].astype(F32) * y
    o_ref[0] = (_rms(y) * og_ref[...]).astype(o_ref.dtype)


def _conv_group(proj3, conv_w, out_g):
    b, t, _ = proj3.shape
    blk = lambda off: pl.BlockSpec((1, t, LANES), lambda bi, g: (bi, 0, off + g))
    return pl.pallas_call(
        _conv_kernel,
        grid=(b, N_HEADS),
        in_specs=[
            blk(0), blk(N_HEADS), blk(2 * N_HEADS),
            pl.BlockSpec((CONV_K, LANES), lambda bi, g: (0, g)),
            pl.BlockSpec((1, LANES), lambda bi, g: (0, g)),
        ],
        out_specs=pl.BlockSpec((1, t, LANES), lambda bi, g: (bi, 0, g)),
        out_shape=jax.ShapeDtypeStruct((b, t, N_HEADS * LANES), BF16),
        compiler_params=pltpu.CompilerParams(
            dimension_semantics=("arbitrary", "arbitrary"), vmem_limit_bytes=VMEM_LIMIT),
        name="conv_group",
    )(proj3, proj3, proj3, conv_w, out_g)


def _attn_kernel(q_ref, k_ref, v_ref, gq_ref, gk_ref, og_ref, o_ref, qs, ks, vs, osc, acc, later, *, t, tp, nh):
    nblk = tp // Q_BLOCK
    pad = tp - t
    hsl = [slice(h * HEAD_DIM, (h + 1) * HEAD_DIM) for h in range(nh)]

    def nrm(x, g):
        return (_rms(x.astype(F32)) * g).astype(BF16)

    for h in range(nh):
        qs[pl.ds(0, t), hsl[h]] = nrm(q_ref[0, :, hsl[h]], gq_ref[...])
        ks[pl.ds(0, t), hsl[h]] = nrm(k_ref[0, :, hsl[h]], gk_ref[...])
    vs[pl.ds(0, t), :] = v_ref[0]
    if pad:
        zpad = jnp.zeros((pad, nh * HEAD_DIM), BF16)
        qs[pl.ds(t, pad), :] = zpad
        ks[pl.ds(t, pad), :] = zpad
        vs[pl.ds(t, pad), :] = zpad

    scale = HEAD_DIM ** -0.5
    r_i = lax.broadcasted_iota(I32, (Q_BLOCK, Q_BLOCK), 0)
    c_i = lax.broadcasted_iota(I32, (Q_BLOCK, Q_BLOCK), 1)
    past = jnp.concatenate([c_i < r_i] * nh, axis=0)
    cum_w = jnp.concatenate([(r_i > c_i).astype(BF16), jnp.ones((Q_BLOCK, Q_BLOCK), BF16)], axis=1)
    cum_w = jnp.concatenate([cum_w, cum_w], axis=0)

    def tiles(qoff, koff, diag):
        z = jnp.concatenate([_nt_dot(qs[pl.ds(qoff, Q_BLOCK), hsl[h]], ks[pl.ds(koff, Q_BLOCK), hsl[h]])
                             for h in range(nh)], axis=0) * scale
        sp = jnp.maximum(z, 0.0) + jnp.log(1.0 + jnp.exp(-jnp.abs(z)))
        log_not = -sp
        if diag:
            log_not = jnp.where(past, log_not, 0.0)
        hi = log_not.astype(BF16)
        lo = (log_not - hi.astype(F32)).astype(BF16)
        cs = jnp.dot(jnp.concatenate([hi, lo], axis=1), cum_w, preferred_element_type=F32)
        rest = cs[:, :Q_BLOCK]
        if not diag:
            rest = rest + later[...]
        w = jnp.exp((z - sp) + rest)
        if diag:
            w = jnp.where(past, w, 0.0)
        wb = w.astype(BF16)
        pv = jnp.concatenate([jnp.dot(wb[h * Q_BLOCK:(h + 1) * Q_BLOCK], vs[pl.ds(koff, Q_BLOCK), hsl[h]],
                                      preferred_element_type=F32) for h in range(nh)], axis=0)
        if diag:
            acc[...] = pv
            later[...] = cs[:, Q_BLOCK:]
        else:
            acc[...] += pv
            later[...] += cs[:, Q_BLOCK:]

    def q_body(qi, carry):
        qoff = pl.multiple_of(qi * Q_BLOCK, Q_BLOCK)
        tiles(qoff, qoff, True)

        def k_body(i, c):
            tiles(qoff, pl.multiple_of((qi - 1 - i) * Q_BLOCK, Q_BLOCK), False)
            return c

        lax.fori_loop(0, qi, k_body, 0)
        for h in range(nh):
            a = acc[h * Q_BLOCK:(h + 1) * Q_BLOCK, :]
            osc[pl.ds(qoff, Q_BLOCK), hsl[h]] = (_rms(a) * og_ref[:, hsl[h]]).astype(osc.dtype)
        return carry

    lax.fori_loop(0, nblk, q_body, 0)
    o_ref[0] = osc[pl.ds(0, t), :]


def _attn_group(proj3, gq, gk, out_g):
    b, t, _ = proj3.shape
    tp = -(-t // Q_BLOCK) * Q_BLOCK
    nh = N_HEADS
    w = nh * HEAD_DIM
    blk = lambda off: pl.BlockSpec((1, t, w), lambda bi: (bi, 0, off))
    return pl.pallas_call(
        functools.partial(_attn_kernel, t=t, tp=tp, nh=nh),
        grid=(b,),
        in_specs=[
            blk(3), blk(4), blk(5),
            pl.BlockSpec((1, HEAD_DIM), lambda bi: (0, 0)),
            pl.BlockSpec((1, HEAD_DIM), lambda bi: (0, 0)),
            pl.BlockSpec((1, w), lambda bi: (0, 1)),
        ],
        out_specs=pl.BlockSpec((1, t, w), lambda bi: (bi, 0, 0)),
        out_shape=jax.ShapeDtypeStruct((b, t, w), BF16),
        scratch_shapes=[pltpu.VMEM((tp, w), BF16)] * 4 + [pltpu.VMEM((nh * Q_BLOCK, Q_BLOCK), F32)] * 2,
        compiler_params=pltpu.CompilerParams(
            dimension_semantics=("arbitrary",), vmem_limit_bytes=VMEM_LIMIT),
        name="attn_group",
    )(proj3, proj3, proj3, gq, gk, out_g)


def _outproj_kernel(cm_ref, am_ref, wa_ref, wb_ref, h_ref, o_ref):
    o_ref[...] = (h_ref[...]
                  + jnp.dot(cm_ref[...], wa_ref[...], preferred_element_type=F32)
                  + jnp.dot(am_ref[...], wb_ref[...], preferred_element_type=F32))


def _outproj(cm, am, w, h, tm, tn):
    n, d = h.shape
    kh = cm.shape[1]
    return pl.pallas_call(
        _outproj_kernel,
        grid=(d // tn, n // tm),
        in_specs=[
            pl.BlockSpec((tm, kh), lambda j, i: (i, 0)),
            pl.BlockSpec((tm, kh), lambda j, i: (i, 0)),
            pl.BlockSpec((kh, tn), lambda j, i: (0, j)),
            pl.BlockSpec((kh, tn), lambda j, i: (1, j)),
            pl.BlockSpec((tm, tn), lambda j, i: (i, j)),
        ],
        out_specs=pl.BlockSpec((tm, tn), lambda j, i: (i, j)),
        out_shape=jax.ShapeDtypeStruct((n, d), F32),
        compiler_params=pltpu.CompilerParams(
            dimension_semantics=("arbitrary", "arbitrary"), vmem_limit_bytes=VMEM_LIMIT,
            allow_input_fusion=[False, False, True, True, True]),
        name="outproj",
    )(cm, am, w, w, h)


def _query_kernel(h_ref, g_ref, w_ref, tok_ref, q_ref):
    xn = (_rms(h_ref[...]) * g_ref[...]).astype(BF16)
    tok_ref[...] = xn
    q_ref[...] = jnp.dot(xn, w_ref[...], preferred_element_type=F32).astype(q_ref.dtype)


def _query(h, g, w, tm, tn):
    n, d = h.shape
    dq = w.shape[1]
    return pl.pallas_call(
        _query_kernel,
        grid=(n // tm, dq // tn),
        in_specs=[
            pl.BlockSpec((tm, d), lambda i, j: (i, 0)),
            pl.BlockSpec((1, d), lambda i, j: (0, 0)),
            pl.BlockSpec((d, tn), lambda i, j: (0, j)),
        ],
        out_specs=[
            pl.BlockSpec((tm, d), lambda i, j: (i, 0)),
            pl.BlockSpec((tm, tn), lambda i, j: (i, j)),
        ],
        out_shape=[jax.ShapeDtypeStruct((n, d), BF16), jax.ShapeDtypeStruct((n, dq), BF16)],
        compiler_params=pltpu.CompilerParams(
            dimension_semantics=("arbitrary", "arbitrary"), vmem_limit_bytes=VMEM_LIMIT),
        name="query",
    )(h, g, w)


def _topk_rows(s, row, n_rows, k):
    vals, idxs = [], []
    for _ in range(k):
        m = jnp.max(s, axis=0, keepdims=True)
        idx = jnp.min(jnp.where(s == m, row, float(n_rows)), axis=0, keepdims=True)
        vals.append(m)
        idxs.append(idx)
        s = jnp.where(row == idx, -jnp.inf, s)
    return vals, idxs


def _stack_rows(rows, row16):
    out = jnp.broadcast_to(rows[0], row16.shape)
    for i in range(1, len(rows)):
        out = jnp.where(row16 == float(i), rows[i], out)
    return out


def _route_kernel(q_ref, sk_ref, e_ref, g_ref, es, gs):
    tm = q_ref.shape[0]
    k = PEER_TOPK
    row_keys = lax.broadcasted_iota(I32, (N_KEYS, tm), 0).astype(F32)
    row16 = lax.broadcasted_iota(I32, (k, tm), 0).astype(F32)
    for h in range(N_HEADS):
        tops = []
        for p in range(2):
            c0 = (2 * h + p) * D_SUBKEY
            s = _nt_dot(sk_ref[p], q_ref[:, c0:c0 + D_SUBKEY])
            tops.append(_topk_rows(s, row_keys, N_KEYS, k))
        (v1, i1), (v2, i2) = tops
        v2s = _stack_rows(v2, row16)
        i2s = _stack_rows(i2, row16)
        half = k // 2
        v1_hi = _stack_rows(v1[half:], row16[:half])
        i1_hi = _stack_rows(i1[half:], row16[:half])
        cand = jnp.concatenate([v1[0] + v2s] + [v1[a] + v2s[:half] for a in range(1, half)]
                               + [v1_hi + v2[0]], axis=0)
        cidx = jnp.concatenate([i1[0] * float(N_KEYS) + i2s]
                               + [i1[a] * float(N_KEYS) + i2s[:half] for a in range(1, half)]
                               + [i1_hi * float(N_KEYS) + i2[0]], axis=0)
        row_cand = jnp.concatenate([row16] + [row16[:half] + float(a * k) for a in range(1, half)]
                                   + [(row16[:half] + float(half)) * float(k)], axis=0)
        best, exps = [], []
        for _ in range(k):
            m = jnp.max(cand, axis=0, keepdims=True)
            pos = jnp.min(jnp.where(cand == m, row_cand, float(k * k)), axis=0, keepdims=True)
            sel = row_cand == pos
            exps.append(jnp.sum(jnp.where(sel, cidx, 0.0), axis=0, keepdims=True))
            best.append(m)
            cand = jnp.where(sel, -jnp.inf, cand)
        ex = [jnp.exp(bv - best[0]) for bv in best]
        den = ex[0]
        for e in ex[1:]:
            den = den + e
        es[h * k:(h + 1) * k, :] = _stack_rows(exps, row16)
        gs[h * k:(h + 1) * k, :] = _stack_rows([e / den for e in ex], row16)
    e_ref[...] = es[...].T.astype(I32)
    g_ref[...] = gs[...].T


def _route(qry, sub_keys):
    n, dq = qry.shape
    tm = TOKEN_BLOCK
    slots = N_HEADS * PEER_TOPK
    return pl.pallas_call(
        _route_kernel,
        grid=(n // tm,),
        in_specs=[
            pl.BlockSpec((tm, dq), lambda i: (i, 0)),
            pl.BlockSpec((2, N_KEYS, D_SUBKEY), lambda i: (0, 0, 0)),
        ],
        out_specs=[
            pl.BlockSpec((tm, slots), lambda i: (i, 0)),
            pl.BlockSpec((tm, slots), lambda i: (i, 0)),
        ],
        out_shape=[jax.ShapeDtypeStruct((n, slots), I32), jax.ShapeDtypeStruct((n, slots), F32)],
        scratch_shapes=[pltpu.VMEM((slots, tm), F32)] * 2,
        compiler_params=pltpu.CompilerParams(
            dimension_semantics=("arbitrary",), vmem_limit_bytes=VMEM_LIMIT),
        name="route",
    )(qry, sub_keys)


def _expert_kernel(ic_ref, itx_ref, ioc_ref, ito_ref, ifl_ref, eoff_ref, eoff_nx_ref, t2_ref, g2_ref, x_ref, uv_ref,
                   o_ref, tu0, tu1, tu2, tv0, tv1, tv2, rs0, rs1, rs2):
    i = pl.program_id(0)
    flags = ifl_ref[i]
    nj = WORDS_PER_EXPERT
    s_rows = GATHER_STRIDE
    tb = x_ref.shape[0]
    row = i % ITEMS_PER_FETCH
    row_nx = (i + 1) % ITEMS_PER_FETCH
    row_pv = (i + ITEMS_PER_FETCH - 1) % ITEMS_PER_FETCH
    tus, tvs, rss = (tu0, tu1, tu2), (tv0, tv1, tv2), (rs0, rs1, rs2)

    def gather(eref, r, tu, tv, p0=0, p1=PAIRS):
        for p in range(p0, p1):
            off = pl.multiple_of(eref[0, r, p], 2 * nj)
            tu[pl.ds(p, nj, stride=s_rows), :] = uv_ref[pl.ds(off, nj), :]
            tv[pl.ds(p, nj, stride=s_rows), :] = uv_ref[pl.ds(off + nj, nj), :]

    def table(tile):
        words = jnp.concatenate([tile[j * s_rows:j * s_rows + PAIRS, :] for j in range(nj)], axis=1)
        return pltpu.bitcast(words, BF16)

    def x_halves():
        x = x_ref[...]
        x_lo = jnp.concatenate([x[:, 2 * j * LANES:(2 * j + 1) * LANES] for j in range(nj)], axis=1)
        x_hi = jnp.concatenate([x[:, (2 * j + 1) * LANES:(2 * j + 2) * LANES] for j in range(nj)], axis=1)
        return jnp.concatenate([x_lo, x_hi], axis=0)

    def coefficients(rs):
        lane = lax.broadcasted_iota(I32, (tb, 2 * PAIRS), 1)
        even = (lane & 1) == 0
        rsel = jnp.where(even, rs[:tb, :], rs[tb:, :])
        mask = lax.broadcasted_iota(I32, (tb, 2 * PAIRS), 0) == t2_ref[0, pl.ds(row_pv, 1), :]
        s2 = jnp.sum(jnp.where(mask, rsel, 0.0), axis=0, keepdims=True)
        s2 = jnp.broadcast_to(s2, (8, 2 * PAIRS))
        even8 = even[:8]
        s = s2 + jnp.where(even8, pltpu.roll(s2, 2 * PAIRS - 1, axis=1), pltpu.roll(s2, 1, axis=1))
        act = 0.5 * s * (1.0 + lax.erf(s * (2.0 ** -0.5)))
        coef = g2_ref[0, pl.ds(row_pv, 1), :] * act[0:1]
        cm = jnp.where(mask, coef, 0.0)
        return jnp.concatenate([jnp.where(even, cm, 0.0), jnp.where(even, 0.0, cm)], axis=0).astype(BF16)

    def accumulate(o):
        for j in range(nj):
            c0 = 2 * j * LANES
            o_ref[0, :, c0:c0 + LANES] += o[:tb, j * LANES:(j + 1) * LANES]
            o_ref[0, :, c0 + LANES:c0 + 2 * LANES] += o[tb:, j * LANES:(j + 1) * LANES]

    def main(k):
        nx, pv = (k + 1) % 3, (k + 2) % 3
        q = PAIRS // GATHER_PIECES
        g = lambda a: gather(eoff_nx_ref, row_nx, tus[nx], tvs[nx], a * q, (a + 1) * q)
        g(0)
        rss[k][...] = _nt_dot(x_halves(), table(tus[k]))
        g(1)
        cf = coefficients(rss[pv])
        g(2)
        o = jnp.dot(cf, table(tvs[pv]), preferred_element_type=F32)
        g(3)
        accumulate(o)

    phase = i % 3
    own = (flags & 1) != 0

    @pl.when(i == 0)
    def _():
        rs2[...] = jnp.zeros(rs2.shape, rs2.dtype)
        tv2[...] = jnp.zeros(tv2.shape, tv2.dtype)

    @pl.when((flags & 2) != 0)
    def _():
        o_ref[...] = jnp.zeros(o_ref.shape, o_ref.dtype)

    for k in range(3):
        @pl.when(own & (phase == k))
        def _(k=k):
            gather(eoff_ref, row, tus[k], tvs[k])

    for k in range(3):
        @pl.when(phase == k)
        def _(k=k):
            main(k)


def _experts(ic, itb, ifl, eoff, t2, g2, tok, uv_pk):
    n, d = tok.shape
    n_items = ic.shape[0]
    chunk_rows = uv_pk.shape[0] // N_CHUNKS
    tb = EXPERT_TOKEN_BLOCK
    nf = ITEMS_PER_FETCH
    true1 = jnp.ones((1,), bool)
    ic_s = jnp.concatenate([ic, ic[-1:]])
    itx_s = jnp.concatenate([itb, itb[-1:]])
    ioc_s = jnp.concatenate([ic[:1], ic])
    ito_s = jnp.concatenate([itb[:1], itb])
    own_u = jnp.concatenate([true1, ic_s[1:] != ic_s[:-1]])
    first_o = jnp.concatenate([true1, (ifl & 2) != 0])
    fl_s = own_u.astype(I32) + 2 * first_o.astype(I32)
    n_rows = -(-(n_items + 2) // nf) * nf
    pad_rows = lambda a: jnp.concatenate(
        [a, jnp.broadcast_to(a[-1:], (n_rows - n_items,) + a.shape[1:])], axis=0).reshape(n_rows // nf, nf, -1)
    eoff3, t23, g23 = pad_rows(eoff), pad_rows(t2), pad_rows(g2)
    blk_cur = lambda i, ic, itx, ioc, ito, fl: (i // nf, 0, 0)
    blk_nxt = lambda i, ic, itx, ioc, ito, fl: ((i + 1) // nf, 0, 0)
    blk_prv = lambda i, ic, itx, ioc, ito, fl: (jnp.maximum(i - 1, 0) // nf, 0, 0)
    grid_spec = pltpu.PrefetchScalarGridSpec(
        num_scalar_prefetch=5,
        grid=(n_items + 1,),
        in_specs=[
            pl.BlockSpec((1, nf, PAIRS), blk_cur, memory_space=pltpu.SMEM),
            pl.BlockSpec((1, nf, PAIRS), blk_nxt, memory_space=pltpu.SMEM),
            pl.BlockSpec((1, nf, 2 * PAIRS), blk_prv),
            pl.BlockSpec((1, nf, 2 * PAIRS), blk_prv),
            pl.BlockSpec((tb, d), lambda i, ic, itx, ioc, ito, fl: (itx[i], 0)),
            pl.BlockSpec((chunk_rows, LANES), lambda i, ic, itx, ioc, ito, fl: (ic[i], 0),
                         pipeline_mode=pl.Buffered(1)),
        ],
        out_specs=pl.BlockSpec((1, tb, d), lambda i, ic, itx, ioc, ito, fl: (ioc[i], ito[i], 0)),
        scratch_shapes=[pltpu.VMEM((WORDS_PER_EXPERT * GATHER_STRIDE, LANES), I32)] * 6
        + [pltpu.VMEM((2 * tb, 2 * PAIRS), F32)] * 3,
    )
    return pl.pallas_call(
        _expert_kernel,
        grid_spec=grid_spec,
        out_shape=jax.ShapeDtypeStruct((N_CHUNKS, n, d), F32),
        compiler_params=pltpu.CompilerParams(
            dimension_semantics=("arbitrary",), vmem_limit_bytes=VMEM_LIMIT),
        name="experts",
    )(ic_s, itx_s, ioc_s, ito_s, fl_s, eoff3, eoff3, t23, g23, tok, uv_pk)


def _combine_kernel(pres_ref, h_ref, p_ref, o_ref, *, ntb, nsb):
    i = pl.program_id(0)
    tb = EXPERT_TOKEN_BLOCK
    for sb in range(nsb):
        rows = slice(sb * tb, (sb + 1) * tb)
        acc = h_ref[rows, :]
        for c in range(N_CHUNKS):
            have = pres_ref[c * ntb + i * nsb + sb] != 0
            acc = acc + jnp.where(have, p_ref[c, rows, :], 0.0)
        o_ref[rows, :] = acc


def _combine(h1, partial, present):
    n, d = h1.shape
    ntb = n // EXPERT_TOKEN_BLOCK
    nsb = next(c for c in (3, 2, 1) if ntb % c == 0)
    tm = nsb * EXPERT_TOKEN_BLOCK
    grid_spec = pltpu.PrefetchScalarGridSpec(
        num_scalar_prefetch=1,
        grid=(n // tm,),
        in_specs=[
            pl.BlockSpec((tm, d), lambda i, pres: (i, 0)),
            pl.BlockSpec((N_CHUNKS, tm, d), lambda i, pres: (0, i, 0)),
        ],
        out_specs=pl.BlockSpec((tm, d), lambda i, pres: (i, 0)),
    )
    return pl.pallas_call(
        functools.partial(_combine_kernel, ntb=ntb, nsb=nsb),
        grid_spec=grid_spec,
        out_shape=jax.ShapeDtypeStruct((n, d), F32),
        compiler_params=pltpu.CompilerParams(
            dimension_semantics=("arbitrary",), vmem_limit_bytes=VMEM_LIMIT),
        name="combine",
    )(present.reshape(-1).astype(I32), h1, partial)


def _pack_kernel(u_ref, v_ref, o_ref):
    r = u_ref.shape[0]
    nj = WORDS_PER_EXPERT
    for t, ref in enumerate((u_ref, v_ref)):
        for j in range(nj):
            lo = ref[:, 2 * j * LANES:(2 * j + 1) * LANES]
            hi = ref[:, (2 * j + 1) * LANES:(2 * j + 2) * LANES]
            words = pltpu.pack_elementwise([lo, hi], packed_dtype=BF16)
            o_ref[pl.ds(t * nj + j, r, stride=2 * nj), :] = pltpu.bitcast(words, I32)


def _pack_tables(u, v):
    e, d = u.shape
    assert d == 2 * LANES * WORDS_PER_EXPERT
    r = PACK_ROWS
    return pl.pallas_call(
        _pack_kernel,
        grid=(e // r,),
        in_specs=[pl.BlockSpec((r, d), lambda i: (i, 0)), pl.BlockSpec((r, d), lambda i: (i, 0))],
        out_specs=pl.BlockSpec((r * 2 * WORDS_PER_EXPERT, LANES), lambda i: (i, 0)),
        out_shape=jax.ShapeDtypeStruct((e * 2 * WORDS_PER_EXPERT, LANES), I32),
        compiler_params=pltpu.CompilerParams(
            dimension_semantics=("arbitrary",), vmem_limit_bytes=VMEM_LIMIT),
        name="pack_tables",
    )(u, v)


def _bucket_pairs(e_t, g_t, n_experts):
    n, slots = e_t.shape
    tblk = EXPERT_TOKEN_BLOCK
    ntb = n // tblk
    per_blk = slots * tblk
    nwin = per_blk // PAIRS
    chunk = n_experts // N_CHUNKS
    blk = lambda a: a.reshape(ntb, per_blk)
    key = blk(e_t) * per_blk + jnp.arange(per_blk, dtype=I32)[None, :]
    key_s, g_s = lax.sort((key, blk(g_t)), dimension=1, num_keys=1, is_stable=False)
    e_s = (key_s // per_blk).reshape(ntb * nwin, PAIRS)
    t_s = ((key_s % per_blk) // slots).reshape(ntb * nwin, PAIRS)
    g_s = g_s.reshape(ntb * nwin, PAIRS)
    cs = jnp.arange(N_CHUNKS, dtype=I32)[:, None, None]
    flag = jnp.any((e_s // chunk)[None] == cs, axis=2)
    n_items = ntb * (nwin + N_CHUNKS - 1)
    ids = jnp.nonzero(flag.reshape(-1), size=n_items, fill_value=-1)[0].astype(I32)
    valid = ids >= 0
    n_valid = jnp.sum(valid.astype(I32))
    ids = jnp.where(valid, ids, ids[jnp.maximum(n_valid - 1, 0)])
    ic = ids // (ntb * nwin)
    iw = ids % (ntb * nwin)
    itb = iw // nwin
    run_id = ic * ntb + itb
    first = jnp.concatenate([jnp.ones((1,), bool), run_id[1:] != run_id[:-1]]) & valid
    ifl = valid.astype(I32) + 2 * first.astype(I32)
    rows = jnp.stack([e_s, t_s, lax.bitcast_convert_type(g_s, I32)], axis=1)[iw]
    e_i, t_i, g_i = rows[:, 0], rows[:, 1], lax.bitcast_convert_type(rows[:, 2], F32)
    in_chunk = (e_i // chunk) == ic[:, None]
    own_e = jnp.max(jnp.where(in_chunk, e_i, -1), axis=1, keepdims=True)
    eoff = (jnp.where(in_chunk, e_i, own_e) - ic[:, None] * chunk) * (2 * WORDS_PER_EXPERT)
    g_i = jnp.where(in_chunk & valid[:, None], g_i, 0.0)
    dup = lambda a: jnp.repeat(a, 2, axis=1)
    present = jnp.any(flag.reshape(N_CHUNKS, ntb, nwin), axis=2)
    return (ic, itb, ifl, eoff, dup(t_i), dup(g_i)), present


def _row_tile(n, candidates):
    for c in candidates:
        if n % c == 0:
            return c
    raise ValueError(f"no row tile for {n}")


def kernel(x, meta_tokens, norm_mix_g, w_in, conv_w, q_norm_g, k_norm_g, out_norm_g, w_out, norm_ffn_g,
           w_query, sub_keys, expert_u, expert_v):
    b, seq, d = x.shape
    t = seq + N_META
    n = b * t
    depth = w_in.shape[0]
    n_experts = expert_u.shape[1]
    assert n % TOKEN_BLOCK == 0 and d == 2 * LANES * WORDS_PER_EXPERT
    tm = _row_tile(n, (688, 512, 256, 128))

    meta = jnp.broadcast_to(meta_tokens.astype(x.dtype)[None], (b, N_META, d))
    h = jnp.concatenate([meta, x], axis=1).reshape(n, d)
    for layer in range(depth):
        proj = _inproj(h, norm_mix_g[layer][None], w_in[layer].astype(BF16), tm, 1024)
        proj3 = proj.reshape(b, t, proj.shape[1])
        og = out_norm_g[layer][None]
        cm = _conv_group(proj3, conv_w[layer], og).reshape(n, -1)
        am = _attn_group(proj3, q_norm_g[layer][None], k_norm_g[layer][None], og).reshape(n, -1)
        h = _outproj(cm, am, w_out[layer].astype(BF16), h, tm, 1024)
        tok, qry = _query(h, norm_ffn_g[layer][None], w_query[layer].astype(BF16), tm, w_query.shape[2])
        e_t, g_t = _route(qry, sub_keys[layer].astype(BF16))
        items, present = _bucket_pairs(e_t, g_t, n_experts)
        partial = _experts(*items, tok, _pack_tables(expert_u[layer], expert_v[layer]))
        h = _combine(h, partial, present)
    return h.reshape(b, t, d)[:, N_META:]
```
